```python
import jax, jax.numpy as jnp
from jax import lax
import numpy as np

D_MODEL = 1024
BATCH = 16
SEQ = 2048
DEPTH = 2
DEC_BATCH = 128
DEC_SEQ = 8
PAST_LEN = 16384
PAGE_SIZE = 128

D_MIX = D_MODEL
D_ATTN = D_MIX // 2
D_HG = D_MIX // 4
D_CONV = D_MIX - D_ATTN - D_HG
HEAD_DIM = 64
N_HEADS = D_ATTN // HEAD_DIM
N_KV_HEADS = 2
D_KV = N_KV_HEADS * HEAD_DIM
WINDOW = 128
ROPE_THETA = 10000.0
HG_DV = 64
HG_HEADS = D_HG // HG_DV
HG_DK = 64
D_FORGET = HG_HEADS * HG_DK
HG_CHUNK = 64
CONV_WIDTH = 3
D_IN = D_ATTN + 2 * D_KV + 2 * D_FORGET + 2 * D_HG + 3 * D_CONV
D_FF = 2816
FFN_RES_WEIGHT = 0.5
N_SUB = 3
RMS_EPS = 1e-6
NEG_BIG = -1e30

kernel_name = 'hymba_hgrn2_swa_conv_macaron_step'

F32 = jnp.float32


def rms_norm(x, g):
    xf = x.astype(F32)
    y = xf * lax.rsqrt(jnp.mean(xf * xf, axis=-1, keepdims=True) + RMS_EPS)
    return (y * g.astype(F32)).astype(x.dtype)


def rope(x, pos):
    half = x.shape[-1] // 2
    inv = jnp.power(ROPE_THETA, -jnp.arange(half, dtype=F32) / half)
    ang = pos.astype(F32)[:, None] * inv[None, :]
    cos = jnp.cos(ang)[None, :, None, :]
    sin = jnp.sin(ang)[None, :, None, :]
    xf = x.astype(F32)
    x1, x2 = xf[..., :half], xf[..., half:]
    return jnp.concatenate([x1 * cos - x2 * sin, x2 * cos + x1 * sin], axis=-1).astype(x.dtype)


def swiglu(h, w_in, w_out):
    g, u = jnp.split(h @ w_in, 2, axis=-1)
    return (jax.nn.silu(g) * u) @ w_out


def sink_attend(q, k, v, q_pos, k_pos, sinks):
    scale = HEAD_DIM ** -0.5
    s = jnp.einsum('bnqhgd,bnkhd->bnhgqk', q, k).astype(F32) * scale
    diff = q_pos[:, :, None] - k_pos[:, None, :]
    mask = (diff >= 0) & (diff < WINDOW) & (k_pos[:, None, :] >= 0)
    s = jnp.where(mask[None, :, None, None], s, NEG_BIG)
    sink = sinks.astype(F32).reshape(N_KV_HEADS, N_HEADS // N_KV_HEADS)[None, None, :, :, None, None]
    m = jnp.maximum(jnp.max(s, axis=-1, keepdims=True), sink)
    p = jnp.exp(s - m)
    p = p / (jnp.sum(p, axis=-1, keepdims=True) + jnp.exp(sink - m))
    return jnp.einsum('bnhgqk,bnkhd->bnqhgd', p.astype(v.dtype), v)


def window_attention(q, k, v, pos0, sinks, kv_prev):
    B, T = q.shape[0], q.shape[1]
    G = N_HEADS // N_KV_HEADS
    if kv_prev is None:
        nb = T // WINDOW
        qb = q.reshape(B, nb, WINDOW, N_KV_HEADS, G, HEAD_DIM)
        kb = k.reshape(B, nb, WINDOW, N_KV_HEADS, HEAD_DIM)
        vb = v.reshape(B, nb, WINDOW, N_KV_HEADS, HEAD_DIM)
        kk = jnp.concatenate([jnp.concatenate([jnp.zeros_like(kb[:, :1]), kb[:, :-1]], axis=1), kb], axis=2)
        vv = jnp.concatenate([jnp.concatenate([jnp.zeros_like(vb[:, :1]), vb[:, :-1]], axis=1), vb], axis=2)
        q_pos = pos0 + jnp.arange(T).reshape(nb, WINDOW)
        k_pos = pos0 + (jnp.arange(nb) * WINDOW)[:, None] + jnp.arange(-WINDOW, WINDOW)[None, :]
        o = sink_attend(qb, kk, vv, q_pos, k_pos, sinks)
        keep = min(WINDOW, T)
        new_k, new_v = k[:, T - keep:], v[:, T - keep:]
    else:
        k_prev, v_prev = kv_prev
        P = k_prev.shape[1]
        k_all = jnp.concatenate([k_prev.astype(k.dtype), k], axis=1)
        v_all = jnp.concatenate([v_prev.astype(v.dtype), v], axis=1)
        q_pos = (pos0 + jnp.arange(T))[None, :]
        k_pos = (pos0 - P + jnp.arange(P + T))[None, :]
        o = sink_attend(q.reshape(B, 1, T, N_KV_HEADS, G, HEAD_DIM), k_all[:, None], v_all[:, None],
                        q_pos, k_pos, sinks)
        new_k, new_v = k_all[:, T:], v_all[:, T:]
    return o.reshape(B, T, D_ATTN), new_k, new_v


def hgrn2_chunked(q, logf, v, S0):
    B, T = q.shape[0], q.shape[1]
    L = min(HG_CHUNK, T)
    nC = -(-T // L)
    pad = nC * L - T
    padw = ((0, 0), (0, pad), (0, 0), (0, 0))
    q = jnp.pad(q, padw)
    logf = jnp.pad(logf, padw)
    v = jnp.pad(v, padw)
    k = -jnp.expm1(logf)

    def to_chunks(a):
        return a.reshape(B, nC, L, HG_HEADS, a.shape[-1]).transpose(1, 0, 3, 2, 4)

    causal = jnp.tril(jnp.ones((L, L), dtype=bool))[None, None, :, :, None]

    def step(S, inp):
        qc, lfc, kc, vc = inp
        b = jnp.cumsum(lfc, axis=2)
        diff = b[:, :, :, None, :] - b[:, :, None, :, :]
        dec = jnp.exp(jnp.where(causal, diff, NEG_BIG))
        A = jnp.einsum('bhtk,bhsk,bhtsk->bhts', qc, kc, dec)
        o = jnp.einsum('bhts,bhsv->bhtv', A, vc) + jnp.einsum('bhtk,bhkv->bhtv', qc * jnp.exp(b), S)
        bL = b[:, :, -1:, :]
        S = jnp.exp(bL[:, :, 0, :])[..., None] * S + jnp.einsum('bhsk,bhsv->bhkv', kc * jnp.exp(bL - b), vc)
        return S, o

    S, o = lax.scan(step, S0, (to_chunks(q), to_chunks(logf), to_chunks(k), to_chunks(v)))
    o = o.transpose(1, 0, 3, 2, 4).reshape(B, nC * L, HG_HEADS, HG_DV)[:, :T]
    return o, S


def short_conv(u, prev, w):
    T = u.shape[1]
    full = jnp.concatenate([prev.astype(u.dtype), u], axis=1)
    y = full[:, 0:T] * w[0]
    for j in range(1, CONV_WIDTH):
        y = y + full[:, j:j + T] * w[j]
    return y, full[:, T:]


def mixer(h, pos0, kv_prev, S0, conv_prev, lb, w_in, w_out, sinks, gnorm, cw):
    B, T = h.shape[0], h.shape[1]
    sizes = [D_ATTN, D_KV, D_KV, D_FORGET, D_FORGET, D_HG, D_HG, D_CONV, D_CONV, D_CONV]
    z = h @ w_in
    qa, ka, va, qh, fh, ih, gh, bc, cc, hc = jnp.split(z, np.cumsum(sizes)[:-1].tolist(), axis=-1)
    pos = pos0 + jnp.arange(T)
    q = rope(qa.reshape(B, T, N_HEADS, HEAD_DIM), pos)
    k = rope(ka.reshape(B, T, N_KV_HEADS, HEAD_DIM), pos)
    v = va.reshape(B, T, N_KV_HEADS, HEAD_DIM)
    o_attn, new_k, new_v = window_attention(q, k, v, pos0, sinks, kv_prev)
    lbf = lb.reshape(HG_HEADS, HG_DK).astype(F32)
    f = lbf + (1.0 - lbf) * jax.nn.sigmoid(fh.reshape(B, T, HG_HEADS, HG_DK).astype(F32))
    logf = jnp.log(f)
    o_hg, S_new = hgrn2_chunked(qh.reshape(B, T, HG_HEADS, HG_DK).astype(F32), logf,
                                ih.reshape(B, T, HG_HEADS, HG_DV).astype(F32), S0.astype(F32))
    o_hg = rms_norm(o_hg, gnorm).reshape(B, T, D_HG).astype(h.dtype) * jax.nn.silu(gh)
    y_conv, conv_new = short_conv(cc * hc, conv_prev, cw)
    y_conv = bc * y_conv
    out = jnp.concatenate([o_attn, o_hg, y_conv], axis=-1) @ w_out
    return out, new_k, new_v, S_new, conv_new


def layer(x, c, pos0, kv_prev, S0, conv_prev, lb, p):
    a_w, a_b, g_pre, g_post, f_in, f_out, m_in, m_out, sinks, gnorm, cw = p
    B = x.shape[0]
    mod = (jax.nn.silu(c) @ a_w + a_b).reshape(B, N_SUB, 3, D_MODEL)[:, :, :, None, :]

    def modulate(y, i):
        return rms_norm(y, g_pre[i]) * (1.0 + mod[:, i, 1]) + mod[:, i, 0]

    def residual(y, out, i, w):
        return y + w * mod[:, i, 2] * rms_norm(out, g_post[i])

    x = residual(x, swiglu(modulate(x, 0), f_in[0], f_out[0]), 0, FFN_RES_WEIGHT)
    mix, new_k, new_v, S_new, conv_new = mixer(modulate(x, 1), pos0, kv_prev, S0, conv_prev, lb,
                                                m_in, m_out, sinks, gnorm, cw)
    x = residual(x, mix, 1, 1.0)
    x = residual(x, swiglu(modulate(x, 2), f_in[1], f_out[1]), 2, FFN_RES_WEIGHT)
    return x, new_k, new_v, S_new, conv_new


def setup_inputs(seed: int = 0) -> dict:
    key = jax.random.key(seed)
    ks = jax.random.split(key, 24)

    def nrm(k, shape, s):
        return jax.random.normal(k, shape, F32) * s

    win_buf = min(WINDOW, PAST_LEN)
    return {
        'x_prompt': nrm(ks[0], (BATCH, SEQ, D_MODEL), 1.0),
        'x_sample': nrm(ks[1], (DEC_BATCH, DEC_SEQ, D_MODEL), 1.0),
        'cache_k_win': nrm(ks[2], (DEPTH, DEC_BATCH, win_buf, N_KV_HEADS, HEAD_DIM), 1.0),
        'cache_v_win': nrm(ks[3], (DEPTH, DEC_BATCH, win_buf, N_KV_HEADS, HEAD_DIM), 1.0),
        'state_hgrn': nrm(ks[4], (DEPTH, DEC_BATCH, HG_HEADS, HG_DK, HG_DV), 0.5),
        'state_conv': nrm(ks[5], (DEPTH, DEC_BATCH, CONV_WIDTH - 1, D_CONV), 1.0),
        'c_prompt': nrm(ks[6], (BATCH, D_MODEL), 1.0),
        'c_sample': nrm(ks[7], (DEC_BATCH, D_MODEL), 1.0),
        'ada_w': nrm(ks[8], (DEPTH, D_MODEL, N_SUB * 3 * D_MODEL), 0.5 * D_MODEL ** -0.5),
        'ada_b': nrm(ks[9], (DEPTH, N_SUB * 3 * D_MODEL), 0.01),
        'norm_pre': 1.0 + nrm(ks[10], (DEPTH, N_SUB, D_MODEL), 0.05),
        'norm_post': 1.0 + nrm(ks[11], (DEPTH, N_SUB, D_MODEL), 0.05),
        'ffn_w_in': nrm(ks[12], (DEPTH, 2, D_MODEL, 2 * D_FF), D_MODEL ** -0.5),
        'ffn_w_out': nrm(ks[13], (DEPTH, 2, D_FF, D_MODEL), D_FF ** -0.5),
        'mix_w_in': nrm(ks[14], (DEPTH, D_MODEL, D_IN), D_MODEL ** -0.5),
        'mix_w_out': nrm(ks[15], (DEPTH, D_MIX, D_MODEL), D_MIX ** -0.5),
        'attn_sinks': nrm(ks[16], (DEPTH, N_HEADS), 0.5),
        'hgrn_lower_bounds': nrm(ks[17], (DEPTH, D_FORGET), 0.5),
        'hgrn_gnorm': 1.0 + nrm(ks[18], (DEPTH, HG_DV), 0.05),
        'conv_w': nrm(ks[19], (DEPTH, CONV_WIDTH, D_CONV), CONV_WIDTH ** -0.5),
    }


def reference(x_prompt, x_sample, cache_k_win, cache_v_win, state_hgrn, state_conv, c_prompt, c_sample,
              ada_w, ada_b, norm_pre, norm_post, ffn_w_in, ffn_w_out, mix_w_in, mix_w_out,
              attn_sinks, hgrn_lower_bounds, hgrn_gnorm, conv_w):
    sm = jax.nn.softmax(hgrn_lower_bounds.astype(F32), axis=0)
    lbs = jnp.cumsum(sm, axis=0) - sm[0:1]

    yp, ys = x_prompt, x_sample
    kp_l, vp_l, sp_l, cp_l = [], [], [], []
    ks_l, vs_l, ss_l, cs_l = [], [], [], []
    for l in range(DEPTH):
        p = (ada_w[l], ada_b[l], norm_pre[l], norm_post[l], ffn_w_in[l], ffn_w_out[l],
             mix_w_in[l], mix_w_out[l], attn_sinks[l], hgrn_gnorm[l], conv_w[l])
        S0p = jnp.zeros((yp.shape[0], HG_HEADS, HG_DK, HG_DV), F32)
        cv0p = jnp.zeros((yp.shape[0], CONV_WIDTH - 1, D_CONV), yp.dtype)
        yp, k1, v1, s1, c1 = layer(yp, c_prompt, 0, None, S0p, cv0p, lbs[l], p)
        ys, k2, v2, s2, c2 = layer(ys, c_sample, PAST_LEN, (cache_k_win[l], cache_v_win[l]),
                                   state_hgrn[l], state_conv[l], lbs[l], p)
        kp_l.append(k1); vp_l.append(v1); sp_l.append(s1.astype(state_hgrn.dtype)); cp_l.append(c1)
        ks_l.append(k2); vs_l.append(v2); ss_l.append(s2.astype(state_hgrn.dtype)); cs_l.append(c2)

    new_k_win_prompt = jnp.stack(kp_l)
    new_v_win_prompt = jnp.stack(vp_l)
    new_state_hgrn_prompt = jnp.stack(sp_l)
    new_state_conv_prompt = jnp.stack(cp_l)
    new_k_win_sample = jnp.stack(ks_l)
    new_v_win_sample = jnp.stack(vs_l)
    new_state_hgrn_sample = jnp.stack(ss_l)
    new_state_conv_sample = jnp.stack(cs_l)
    return (yp, ys, new_k_win_prompt, new_v_win_prompt, new_state_hgrn_prompt, new_state_conv_prompt,
            new_k_win_sample, new_v_win_sample, new_state_hgrn_sample, new_state_conv_sample)
```

```python
import functools

import jax
import jax.numpy as jnp
from jax import lax
from jax.experimental import pallas as pl
from jax.experimental.pallas import tpu as pltpu

F32 = jnp.float32
BF16 = jnp.bfloat16

D_MODEL = 1024
DEPTH = 2
PAST_LEN = 16384
HEAD_DIM = 64
N_HEADS = 8
N_KV_HEADS = 2
GROUP = N_HEADS // N_KV_HEADS
WINDOW = 128
ROPE_THETA = 10000.0
D_ATTN = N_HEADS * HEAD_DIM
D_KV = N_KV_HEADS * HEAD_DIM
D_HG = 256
HG_HEADS = 4
HG_CHUNK = 64
HG_DIAG = 8
D_CONV = 256
CONV_WIDTH = 3
D_IN = 2560
D_FF = 2816
FFN_RES_WEIGHT = 0.5
RMS_EPS = 1e-6
NEG_BIG = -1e30

OFF_QA, OFF_KA, OFF_VA = 0, 512, 640
OFF_QH, OFF_FH, OFF_IH, OFF_GH = 768, 1024, 1280, 1536
OFF_BC, OFF_CC, OFF_HC = 1792, 2048, 2304

FF_CHUNK = 256
VMEM_LIMIT = 56 * 1024 * 1024


def _sigmoid(x):
    return 1.0 / (1.0 + jnp.exp(-x))


def _rms(x, g):
    ms = jnp.mean(x * x, axis=-1, keepdims=True)
    return x * lax.rsqrt(ms + RMS_EPS) * g


def _dot(a, b):
    return jnp.dot(a, b, preferred_element_type=F32)


def _dot_nt(a, b):
    return lax.dot_general(a, b, (((1,), (1,)), ((), ())), preferred_element_type=F32)


def _dot_tn(a, b):
    return lax.dot_general(a, b, (((0,), (0,)), ((), ())), preferred_element_type=F32)


def _adaln_kernel(c_ref, w_ref, b_ref, o_ref):
    c = c_ref[...]
    sc = (c * _sigmoid(c)).astype(BF16)
    o_ref[...] = _dot(sc, w_ref[...].astype(BF16)) + b_ref[...]


def _adaln(c_all, ada_w, ada_b):
    nb = c_all.shape[0]
    n_out = ada_w.shape[-1]
    tn = 1024
    return pl.pallas_call(
        _adaln_kernel,
        grid=(DEPTH, n_out // tn),
        in_specs=[
            pl.BlockSpec((nb, D_MODEL), lambda l, n: (0, 0)),
            pl.BlockSpec((None, D_MODEL, tn), lambda l, n: (l, 0, n)),
            pl.BlockSpec((None, 1, tn), lambda l, n: (l, 0, n)),
        ],
        out_specs=pl.BlockSpec((None, nb, tn), lambda l, n: (l, 0, n)),
        out_shape=jax.ShapeDtypeStruct((DEPTH, nb, n_out), F32),
        compiler_params=pltpu.CompilerParams(
            dimension_semantics=("parallel", "parallel"), vmem_limit_bytes=VMEM_LIMIT),
        name="adaln",
    )(c_all, ada_w, ada_b.reshape(DEPTH, 1, n_out))


def _mod_spec(mod3, tm, tiles_per_batch):
    if mod3.ndim == 4:
        return pl.BlockSpec((None, 3, 1, D_MODEL), lambda i: (i // tiles_per_batch, 0, 0, 0))
    return pl.BlockSpec((3, tm, D_MODEL), lambda i: (0, i, 0))


def _const_spec(shape):
    nd = len(shape)
    return pl.BlockSpec(shape, lambda i: (0,) * nd, pipeline_mode=pl.Buffered(1))


def _modulate(x, mod_ref, gpre_ref):
    return _rms(x, gpre_ref[...]) * (1.0 + mod_ref[1]) + mod_ref[0]


def _ffn_kernel(x_ref, mod_ref, gpre_ref, gpost_ref, win_ref, wout_ref, o_ref):
    x = x_ref[...]
    hb = _modulate(x, mod_ref, gpre_ref).astype(BF16)
    acc = jnp.zeros(x.shape, F32)
    for c in range(D_FF // FF_CHUNK):
        lo = c * FF_CHUNK
        g = _dot(hb, win_ref[:, lo:lo + FF_CHUNK])
        u = _dot(hb, win_ref[:, D_FF + lo:D_FF + lo + FF_CHUNK])
        a = (g * _sigmoid(g) * u).astype(BF16)
        acc = acc + _dot(a, wout_ref[lo:lo + FF_CHUNK, :])
    o_ref[...] = x + (FFN_RES_WEIGHT * mod_ref[2]) * _rms(acc, gpost_ref[...])


def _ffn(x2d, mod3, g_pre, g_post, w_in, w_out, tm, tiles_per_batch):
    ntok = x2d.shape[0]
    return pl.pallas_call(
        _ffn_kernel,
        grid=(ntok // tm,),
        in_specs=[
            pl.BlockSpec((tm, D_MODEL), lambda i: (i, 0)),
            _mod_spec(mod3, tm, tiles_per_batch),
            _const_spec((1, D_MODEL)),
            _const_spec((1, D_MODEL)),
            _const_spec((D_MODEL, 2 * D_FF)),
            _const_spec((D_FF, D_MODEL)),
        ],
        out_specs=pl.BlockSpec((tm, D_MODEL), lambda i: (i, 0)),
        out_shape=jax.ShapeDtypeStruct((ntok, D_MODEL), F32),
        compiler_params=pltpu.CompilerParams(
            dimension_semantics=("parallel",), vmem_limit_bytes=VMEM_LIMIT),
        name="ffn",
    )(x2d, mod3, g_pre, g_post, w_in, w_out)


def _inproj_kernel(x_ref, mod_ref, gpre_ref, w_ref, z_ref):
    hb = _modulate(x_ref[...], mod_ref, gpre_ref).astype(BF16)
    z_ref[...] = _dot(hb, w_ref[...])


def _inproj(x2d, mod3, g_pre, w_in, tm, tiles_per_batch):
    ntok = x2d.shape[0]
    return pl.pallas_call(
        _inproj_kernel,
        grid=(ntok // tm,),
        in_specs=[
            pl.BlockSpec((tm, D_MODEL), lambda i: (i, 0)),
            _mod_spec(mod3, tm, tiles_per_batch),
            _const_spec((1, D_MODEL)),
            _const_spec((D_MODEL, D_IN)),
        ],
        out_specs=pl.BlockSpec((tm, D_IN), lambda i: (i, 0)),
        out_shape=jax.ShapeDtypeStruct((ntok, D_IN), F32),
        compiler_params=pltpu.CompilerParams(
            dimension_semantics=("parallel",), vmem_limit_bytes=VMEM_LIMIT),
        name="mix_in",
    )(x2d, mod3, g_pre, w_in)


def _outproj_kernel(x_ref, m_ref, mod_ref, gpost_ref, w_ref, o_ref):
    out = _dot(m_ref[...].astype(BF16), w_ref[...])
    o_ref[...] = x_ref[...] + mod_ref[2] * _rms(out, gpost_ref[...])


def _outproj(x2d, mixed, mod3, g_post, w_out, tm, tiles_per_batch):
    ntok = x2d.shape[0]
    return pl.pallas_call(
        _outproj_kernel,
        grid=(ntok // tm,),
        in_specs=[
            pl.BlockSpec((tm, D_MODEL), lambda i: (i, 0)),
            pl.BlockSpec((tm, D_MODEL), lambda i: (i, 0)),
            _mod_spec(mod3, tm, tiles_per_batch),
            _const_spec((1, D_MODEL)),
            _const_spec((D_MODEL, D_MODEL)),
        ],
        out_specs=pl.BlockSpec((tm, D_MODEL), lambda i: (i, 0)),
        out_shape=jax.ShapeDtypeStruct((ntok, D_MODEL), F32),
        compiler_params=pltpu.CompilerParams(
            dimension_semantics=("parallel",), vmem_limit_bytes=VMEM_LIMIT),
        name="mix_out",
    )(x2d, mixed, mod3, g_post, w_out)


def _rope_tables(pos):
    half = HEAD_DIM // 2
    inv = jnp.power(ROPE_THETA, -jnp.arange(half, dtype=F32) / half)
    ang = pos.astype(F32)[:, None] * inv[None, :]
    cos, sin = jnp.cos(ang), jnp.sin(ang)
    cs = jnp.tile(jnp.concatenate([cos, cos], axis=-1), (1, N_HEADS))
    sn = jnp.tile(jnp.concatenate([-sin, sin], axis=-1), (1, N_HEADS))
    return cs, sn


def _rope(x, cs, sn):
    w = x.shape[-1]
    lane = lax.broadcasted_iota(jnp.int32, x.shape, 1)
    swapped = jnp.where((lane & (HEAD_DIM - 1)) < HEAD_DIM // 2,
                        pltpu.roll(x, w - HEAD_DIM // 2, 1), pltpu.roll(x, HEAD_DIM // 2, 1))
    return x * cs + swapped * sn


def _lower_bound(lbraw, layer):
    rows = [lbraw[i:i + 1, :] for i in range(DEPTH)]
    mx = functools.reduce(jnp.maximum, rows)
    es = [jnp.exp(r - mx) for r in rows]
    tot = functools.reduce(lambda a, b: a + b, es)
    sm = [e / tot for e in es]
    acc = sm[0]
    for i in range(1, layer + 1):
        acc = acc + sm[i]
    return acc - sm[0]


def _div_pow2(x, n):
    assert n & (n - 1) == 0
    return x >> (n.bit_length() - 1)


def _same_head(nrows, rows_per_head):
    r = _div_pow2(lax.broadcasted_iota(jnp.int32, (nrows, D_HG), 0), rows_per_head)
    c = _div_pow2(lax.broadcasted_iota(jnp.int32, (nrows, D_HG), 1), HEAD_DIM)
    return r == c


def _head_sum(w, head_ones):
    return _dot(w.astype(BF16), head_ones)


def _cumsum_rows(x, period):
    rmod = lax.broadcasted_iota(jnp.int32, x.shape, 0) & (period - 1)
    s = 1
    while s < period:
        x = x + jnp.where(rmod >= s, pltpu.roll(x, s, 0), 0.0)
        s *= 2
    return x


def _hgrn_diag(qh, kk, f, vv, head_ones):
    rmod = lax.broadcasted_iota(jnp.int32, qh.shape, 0) & (HG_DIAG - 1)
    o = _head_sum(qh * kk, head_ones) * vv
    prod = None
    fr = f
    for d in range(1, HG_DIAG):
        prod = fr if prod is None else prod * fr
        w = jnp.where(rmod >= d, qh * pltpu.roll(kk, d, 0) * prod, 0.0)
        o = o + _head_sum(w, head_ones) * pltpu.roll(vv, d, 0)
        fr = pltpu.roll(fr, 1, 0)
    return o


def _hgrn_chunk(qh, kk, b, vv, state, levels):
    rows = qh.shape[0]
    o = _dot((qh * jnp.exp(b)).astype(BF16), state.astype(BF16))
    if levels:
        row = lax.broadcasted_iota(jnp.int32, (rows, D_HG), 0)
        col_s = lax.broadcasted_iota(jnp.int32, (rows, 4 * rows), 1) & (rows - 1)
        row_a = lax.broadcasted_iota(jnp.int32, (rows, 4 * rows), 0)
        bd = _same_head(4 * rows, rows)
        a = jnp.zeros((rows, 4 * rows), F32)
        for m in levels:
            ref = jnp.concatenate(
                [jnp.broadcast_to(b[i * 2 * m + m - 1:i * 2 * m + m, :], (2 * m, D_HG))
                 for i in range(rows // (2 * m))], axis=0)
            upper = (row & (2 * m - 1)) >= m
            qt = jnp.where(upper, qh * jnp.exp(jnp.minimum(b - ref, 0.0)), 0.0)
            kt = jnp.where(upper, 0.0, kk * jnp.exp(jnp.minimum(ref - b, 0.0)))
            kbd = jnp.where(bd, jnp.concatenate([kt] * 4, axis=0), 0.0).astype(BF16)
            am = _dot_nt(qt.astype(BF16), kbd)
            a = a + jnp.where(_div_pow2(row_a, 2 * m) == _div_pow2(col_s, 2 * m), am, 0.0)
        vbd = jnp.where(bd, jnp.concatenate([vv] * 4, axis=0), 0.0).astype(BF16)
        o = o + _dot(a.astype(BF16), vbd)
    b_last = b[rows - 1:rows, :]
    khat = kk * jnp.exp(b_last - b)
    upd = _dot_tn(khat.astype(BF16), vv.astype(BF16))
    sq = (D_HG, D_HG)
    eye = lax.broadcasted_iota(jnp.int32, sq, 0) == lax.broadcasted_iota(jnp.int32, sq, 1)
    decay_col = jnp.sum(jnp.where(eye, jnp.broadcast_to(jnp.exp(b_last), sq), 0.0),
                        axis=1, keepdims=True)
    new_state = decay_col * state + jnp.where(_same_head(D_HG, HEAD_DIM), upd, 0.0)
    return o, new_state


def _hgrn_gates(z, lb):
    f = lb + (1.0 - lb) * _sigmoid(z[:, OFF_FH:OFF_FH + D_HG])
    return z[:, OFF_QH:OFF_QH + D_HG], 1.0 - f, f, jnp.log(f), z[:, OFF_IH:OFF_IH + D_HG]


def _hgrn_finish(o, gh, gn, head_ones):
    sq = o * o
    hi = sq.astype(BF16)
    lo = (sq - hi.astype(F32)).astype(BF16)
    ms = (_dot(hi, head_ones) + _dot(lo, head_ones)) * (1.0 / HEAD_DIM)
    return o * lax.rsqrt(ms + RMS_EPS) * gn * (gh * _sigmoid(gh))


def _softmax_pv(parts, sink):
    masked = [jnp.where(msk, s, NEG_BIG) for s, msk, _ in parts]
    m = sink
    for s in masked:
        m = jnp.maximum(m, jnp.max(s, axis=-1, keepdims=True))
    den = jnp.exp(sink - m)
    acc = None
    for s, (_, _, v) in zip(masked, parts):
        p = jnp.exp(s - m)
        den = den + jnp.sum(p, axis=-1, keepdims=True)
        pv = _dot(p.astype(BF16), v)
        acc = pv if acc is None else acc + pv
    return acc * (1.0 / den)


def _mixp_kernel(sink_ref, z_ref, cs_ref, sn_ref, lb_ref, gn_ref, cw_ref,
                 mix_ref, nk_ref, nv_ref, ns_ref, nc_ref,
                 kbuf, vbuf, sbuf, ubuf, *, layer, tt, nt):
    t = pl.program_id(1)
    z = z_ref[...]
    cs, sn = cs_ref[...], sn_ref[...]

    q = (_rope(z[:, OFF_QA:OFF_QA + D_ATTN], cs, sn) * (HEAD_DIM ** -0.5)).astype(BF16)
    k = _rope(z[:, OFF_KA:OFF_KA + D_KV], cs[:, :D_KV], sn[:, :D_KV])
    v = z[:, OFF_VA:OFF_VA + D_KV]

    @pl.when(t == 0)
    def _():
        kbuf[0:WINDOW, :] = jnp.zeros((WINDOW, D_KV), BF16)
        vbuf[0:WINDOW, :] = jnp.zeros((WINDOW, D_KV), BF16)
        sbuf[...] = jnp.zeros(sbuf.shape, F32)
        ubuf[0:8, :] = jnp.zeros((8, D_CONV), F32)

    @pl.when(t > 0)
    def _():
        kbuf[0:WINDOW, :] = kbuf[tt:tt + WINDOW, :]
        vbuf[0:WINDOW, :] = vbuf[tt:tt + WINDOW, :]
        ubuf[0:8, :] = ubuf[tt:tt + 8, :]

    kbuf[WINDOW:WINDOW + tt, :] = k.astype(BF16)
    vbuf[WINDOW:WINDOW + tt, :] = v.astype(BF16)

    @pl.when(t == nt - 1)
    def _():
        nk_ref[...] = k[tt - WINDOW:, :]
        nv_ref[...] = v[tt - WINDOW:, :]

    qi = lax.broadcasted_iota(jnp.int32, (WINDOW, 2 * WINDOW), 0)
    kj = lax.broadcasted_iota(jnp.int32, (WINDOW, 2 * WINDOW), 1)
    band = (kj > qi) & (kj <= qi + WINDOW)
    first_kmin = jnp.where(t == 0, WINDOW, 0)
    for j in range(tt // WINDOW):
        mask = (band & (kj >= first_kmin)) if j == 0 else band
        r0 = j * WINDOW
        for hk in range(N_KV_HEADS):
            kk_ = kbuf[r0:r0 + 2 * WINDOW, hk * HEAD_DIM:(hk + 1) * HEAD_DIM]
            vv_ = vbuf[r0:r0 + 2 * WINDOW, hk * HEAD_DIM:(hk + 1) * HEAD_DIM]
            for g in range(GROUP):
                h = hk * GROUP + g
                s = _dot_nt(q[r0:r0 + WINDOW, h * HEAD_DIM:(h + 1) * HEAD_DIM], kk_)
                mix_ref[r0:r0 + WINDOW, h * HEAD_DIM:(h + 1) * HEAD_DIM] = _softmax_pv(
                    [(s, mask, vv_)], sink_ref[h])

    head_ones = _same_head(D_HG, HEAD_DIM).astype(BF16)
    lb = _lower_bound(lb_ref[...], layer)
    qh, kk, f, lf, vv = _hgrn_gates(z, lb)
    b = _cumsum_rows(lf, HG_CHUNK)
    o_hg = _hgrn_diag(qh, kk, f, vv, head_ones)
    state = sbuf[...]
    outs = []
    levels = tuple(m for m in (32, 16, 8) if m >= HG_DIAG)
    for c in range(tt // HG_CHUNK):
        sl = slice(c * HG_CHUNK, (c + 1) * HG_CHUNK)
        o_c, state = _hgrn_chunk(qh[sl], kk[sl], b[sl], vv[sl], state, levels)
        outs.append(o_c)
    sbuf[...] = state
    o_hg = o_hg + jnp.concatenate(outs, axis=0)
    mix_ref[:, D_ATTN:D_ATTN + D_HG] = _hgrn_finish(
        o_hg, z[:, OFF_GH:OFF_GH + D_HG], gn_ref[...], head_ones)

    @pl.when(t == nt - 1)
    def _():
        for h in range(HG_HEADS):
            ns_ref[h] = state[h * HEAD_DIM:(h + 1) * HEAD_DIM, h * HEAD_DIM:(h + 1) * HEAD_DIM]

    u = z[:, OFF_CC:OFF_CC + D_CONV] * z[:, OFF_HC:OFF_HC + D_CONV]
    ubuf[8:8 + tt, :] = u
    cw = cw_ref[...]
    y = ubuf[6:6 + tt, :] * cw[0:1, :] + ubuf[7:7 + tt, :] * cw[1:2, :] + u * cw[2:3, :]
    mix_ref[:, D_ATTN + D_HG:] = z[:, OFF_BC:OFF_BC + D_CONV] * y

    @pl.when(t == nt - 1)
    def _():
        nc_ref[...] = u[tt - (CONV_WIDTH - 1):, :]


def _mix_prompt(z3, cs, sn, sinks, lbraw, gn, cw, layer, tt):
    bsz, seq, _ = z3.shape
    nt = seq // tt
    kern = functools.partial(_mixp_kernel, layer=layer, tt=tt, nt=nt)
    per_b = lambda b, t: (b, 0, 0)
    return pl.pallas_call(
        kern,
        grid=(bsz, nt),
        in_specs=[
            pl.BlockSpec(memory_space=pltpu.SMEM),
            pl.BlockSpec((None, tt, D_IN), lambda b, t: (b, t, 0)),
            pl.BlockSpec((tt, D_ATTN), lambda b, t: (t, 0)),
            pl.BlockSpec((tt, D_ATTN), lambda b, t: (t, 0)),
            pl.BlockSpec((DEPTH, D_HG), lambda b, t: (0, 0)),
            pl.BlockSpec((1, D_HG), lambda b, t: (0, 0)),
            pl.BlockSpec((CONV_WIDTH, D_CONV), lambda b, t: (0, 0)),
        ],
        out_specs=[
            pl.BlockSpec((None, tt, D_MODEL), lambda b, t: (b, t, 0)),
            pl.BlockSpec((None, WINDOW, D_KV), per_b),
            pl.BlockSpec((None, WINDOW, D_KV), per_b),
            pl.BlockSpec((None, HG_HEADS, HEAD_DIM, HEAD_DIM), lambda b, t: (b, 0, 0, 0)),
            pl.BlockSpec((None, CONV_WIDTH - 1, D_CONV), per_b),
        ],
        out_shape=[
            jax.ShapeDtypeStruct((bsz, seq, D_MODEL), F32),
            jax.ShapeDtypeStruct((bsz, WINDOW, D_KV), F32),
            jax.ShapeDtypeStruct((bsz, WINDOW, D_KV), F32),
            jax.ShapeDtypeStruct((bsz, HG_HEADS, HEAD_DIM, HEAD_DIM), F32),
            jax.ShapeDtypeStruct((bsz, CONV_WIDTH - 1, D_CONV), F32),
        ],
        scratch_shapes=[
            pltpu.VMEM((WINDOW + tt, D_KV), BF16),
            pltpu.VMEM((WINDOW + tt, D_KV), BF16),
            pltpu.VMEM((D_HG, D_HG), F32),
            pltpu.VMEM((8 + tt, D_CONV), F32),
        ],
        compiler_params=pltpu.CompilerParams(
            dimension_semantics=("parallel", "arbitrary"), vmem_limit_bytes=VMEM_LIMIT),
        name="mix_prompt",
    )(sinks, z3, cs, sn, lbraw, gn, cw)


SAMPLE_ROWS = 16


def _pad_rows(x, fill):
    return jnp.concatenate([x, jnp.full((SAMPLE_ROWS - x.shape[0], x.shape[1]), fill, x.dtype)],
                           axis=0)


def _mixs_kernel(sink_ref, z_ref, cs_ref, sn_ref, lb_ref, gn_ref, cw_ref,
                 ck_ref, cv_ref, s0_ref, c0_ref,
                 mix_ref, nk_ref, nv_ref, ns_ref, nc_ref, ubuf, *, layer, tq, past):
    z = z_ref[...]
    cs, sn = cs_ref[...], sn_ref[...]

    q = (_rope(z[:, OFF_QA:OFF_QA + D_ATTN], cs, sn) * (HEAD_DIM ** -0.5)).astype(BF16)
    k = _rope(z[:, OFF_KA:OFF_KA + D_KV], cs[:, :D_KV], sn[:, :D_KV])
    v = z[:, OFF_VA:OFF_VA + D_KV]
    ck, cv = ck_ref[...], cv_ref[...]
    nk_ref[0:past - tq, :] = ck[tq:, :]
    nk_ref[past - tq:, :] = k
    nv_ref[0:past - tq, :] = cv[tq:, :]
    nv_ref[past - tq:, :] = v
    ckb, cvb = ck.astype(BF16), cv.astype(BF16)
    kpb, vpb = _pad_rows(k, 0.0).astype(BF16), _pad_rows(v, 0.0).astype(BF16)

    mrows = GROUP * tq
    qi_c = lax.broadcasted_iota(jnp.int32, (mrows, past), 0) & (tq - 1)
    kj_c = lax.broadcasted_iota(jnp.int32, (mrows, past), 1)
    mask_c = kj_c > qi_c + (past - WINDOW)
    qi_n = lax.broadcasted_iota(jnp.int32, (mrows, SAMPLE_ROWS), 0) & (tq - 1)
    kj_n = lax.broadcasted_iota(jnp.int32, (mrows, SAMPLE_ROWS), 1)
    mask_n = kj_n <= qi_n
    grp = _div_pow2(lax.broadcasted_iota(jnp.int32, (mrows, 1), 0), tq)
    for hk in range(N_KV_HEADS):
        hs = slice(hk * HEAD_DIM, (hk + 1) * HEAD_DIM)
        qs = jnp.concatenate(
            [q[:, (hk * GROUP + g) * HEAD_DIM:(hk * GROUP + g + 1) * HEAD_DIM]
             for g in range(GROUP)], axis=0)
        sink = jnp.zeros((mrows, 1), F32)
        for g in range(GROUP):
            sink = jnp.where(grp == g, sink_ref[hk * GROUP + g], sink)
        o = _softmax_pv([(_dot_nt(qs, ckb[:, hs]), mask_c, cvb[:, hs]),
                         (_dot_nt(qs, kpb[:, hs]), mask_n, vpb[:, hs])], sink)
        for g in range(GROUP):
            h = hk * GROUP + g
            mix_ref[:, h * HEAD_DIM:(h + 1) * HEAD_DIM] = o[g * tq:(g + 1) * tq, :]

    head_ones = _same_head(D_HG, HEAD_DIM).astype(BF16)
    lb = _lower_bound(lb_ref[...], layer)
    qh, kk, f, lf, vv = _hgrn_gates(z, lb)
    qh, kk, lf, vv = (_pad_rows(a, 0.0) for a in (qh, kk, lf, vv))
    f = _pad_rows(f, 1.0)
    b = _cumsum_rows(lf, SAMPLE_ROWS)
    s0 = s0_ref[...].reshape(D_HG, HEAD_DIM)
    state = jnp.where(_same_head(D_HG, HEAD_DIM), jnp.concatenate([s0] * HG_HEADS, axis=1), 0.0)
    o_c, state = _hgrn_chunk(qh, kk, b, vv, state, ())
    o_hg = (_hgrn_diag(qh, kk, f, vv, head_ones) + o_c)[0:tq, :]
    mix_ref[:, D_ATTN:D_ATTN + D_HG] = _hgrn_finish(
        o_hg, z[:, OFF_GH:OFF_GH + D_HG], gn_ref[...], head_ones)
    for h in range(HG_HEADS):
        ns_ref[h] = state[h * HEAD_DIM:(h + 1) * HEAD_DIM, h * HEAD_DIM:(h + 1) * HEAD_DIM]

    u = z[:, OFF_CC:OFF_CC + D_CONV] * z[:, OFF_HC:OFF_HC + D_CONV]
    ubuf[0:6, :] = jnp.zeros((6, D_CONV), F32)
    ubuf[6:8, :] = c0_ref[...]
    ubuf[8:8 + tq, :] = u
    cw = cw_ref[...]
    y = ubuf[6:6 + tq, :] * cw[0:1, :] + ubuf[7:7 + tq, :] * cw[1:2, :] + u * cw[2:3, :]
    mix_ref[:, D_ATTN + D_HG:] = z[:, OFF_BC:OFF_BC + D_CONV] * y
    nc_ref[...] = u[tq - (CONV_WIDTH - 1):, :]


def _mix_sample(z3, cs, sn, sinks, lbraw, gn, cw, cache_k, cache_v, s0, c0, layer):
    bsz, tq, _ = z3.shape
    past = cache_k.shape[1]
    kern = functools.partial(_mixs_kernel, layer=layer, tq=tq, past=past)
    per_b = lambda b: (b, 0, 0)
    shared = lambda b: (0, 0)
    return pl.pallas_call(
        kern,
        grid=(bsz,),
        in_specs=[
            pl.BlockSpec(memory_space=pltpu.SMEM),
            pl.BlockSpec((None, tq, D_IN), per_b),
            pl.BlockSpec((tq, D_ATTN), shared),
            pl.BlockSpec((tq, D_ATTN), shared),
            pl.BlockSpec((DEPTH, D_HG), shared),
            pl.BlockSpec((1, D_HG), shared),
            pl.BlockSpec((CONV_WIDTH, D_CONV), shared),
            pl.BlockSpec((None, past, D_KV), per_b),
            pl.BlockSpec((None, past, D_KV), per_b),
            pl.BlockSpec((None, HG_HEADS, HEAD_DIM, HEAD_DIM), lambda b: (b, 0, 0, 0)),
            pl.BlockSpec((None, CONV_WIDTH - 1, D_CONV), per_b),
        ],
        out_specs=[
            pl.BlockSpec((None, tq, D_MODEL), per_b),
            pl.BlockSpec((None, past, D_KV), per_b),
            pl.BlockSpec((None, past, D_KV), per_b),
            pl.BlockSpec((None, HG_HEADS, HEAD_DIM, HEAD_DIM), lambda b: (b, 0, 0, 0)),
            pl.BlockSpec((None, CONV_WIDTH - 1, D_CONV), per_b),
        ],
        out_shape=[
            jax.ShapeDtypeStruct((bsz, tq, D_MODEL), F32),
            jax.ShapeDtypeStruct((bsz, past, D_KV), F32),
            jax.ShapeDtypeStruct((bsz, past, D_KV), F32),
            jax.ShapeDtypeStruct((bsz, HG_HEADS, HEAD_DIM, HEAD_DIM), F32),
            jax.ShapeDtypeStruct((bsz, CONV_WIDTH - 1, D_CONV), F32),
        ],
        scratch_shapes=[pltpu.VMEM((8 + tq, D_CONV), F32)],
        compiler_params=pltpu.CompilerParams(
            dimension_semantics=("parallel",), vmem_limit_bytes=VMEM_LIMIT),
        name="mix_sample",
    )(sinks, z3, cs, sn, lbraw, gn, cw, cache_k, cache_v, s0, c0)


PROMPT_TM = 512
SAMPLE_TM = 256
PROMPT_TT = 256


def kernel(x_prompt, x_sample, cache_k_win, cache_v_win, state_hgrn, state_conv, c_prompt, c_sample,
           ada_w, ada_b, norm_pre, norm_post, ffn_w_in, ffn_w_out, mix_w_in, mix_w_out,
           attn_sinks, hgrn_lower_bounds, hgrn_gnorm, conv_w):
    bp, seq, _ = x_prompt.shape
    bs, tq, _ = x_sample.shape
    past = cache_k_win.shape[2]
    tiles_per_batch = seq // PROMPT_TM

    mod = _adaln(jnp.concatenate([c_prompt, c_sample], axis=0), ada_w, ada_b)
    mod = mod.reshape(DEPTH, bp + bs, 3, 3, D_MODEL)
    cs_p, sn_p = _rope_tables(jnp.arange(seq))
    cs_s, sn_s = _rope_tables(PAST_LEN + jnp.arange(tq))
    gn = jnp.tile(hgrn_gnorm, (1, HG_HEADS)).reshape(DEPTH, 1, D_HG)

    yp = x_prompt.reshape(bp * seq, D_MODEL)
    ys = x_sample.reshape(bs * tq, D_MODEL)
    outs_p, outs_s = [], []
    for l in range(DEPTH):
        modp = mod[l, :bp].reshape(bp, 3, 3, 1, D_MODEL)
        mods = jnp.repeat(mod[l, bp:], tq, axis=0).transpose(1, 2, 0, 3)
        g_pre = norm_pre[l].reshape(3, 1, D_MODEL)
        g_post = norm_post[l].reshape(3, 1, D_MODEL)
        w_ffn_in = ffn_w_in[l].astype(BF16)
        w_ffn_out = ffn_w_out[l].astype(BF16)
        w_mix_in = mix_w_in[l].astype(BF16)
        w_mix_out = mix_w_out[l].astype(BF16)

        def ffn(x, mod3, i, j, tm):
            return _ffn(x, mod3, g_pre[i], g_post[i], w_ffn_in[j], w_ffn_out[j], tm, tiles_per_batch)

        yp = ffn(yp, modp[:, 0], 0, 0, PROMPT_TM)
        ys = ffn(ys, mods[0], 0, 0, SAMPLE_TM)

        zp = _inproj(yp, modp[:, 1], g_pre[1], w_mix_in, PROMPT_TM, tiles_per_batch)
        zs = _inproj(ys, mods[1], g_pre[1], w_mix_in, SAMPLE_TM, tiles_per_batch)
        mp = _mix_prompt(zp.reshape(bp, seq, D_IN), cs_p, sn_p, attn_sinks[l], hgrn_lower_bounds,
                         gn[l], conv_w[l], l, PROMPT_TT)
        ms = _mix_sample(zs.reshape(bs, tq, D_IN), cs_s, sn_s, attn_sinks[l], hgrn_lower_bounds,
                         gn[l], conv_w[l], cache_k_win[l].reshape(bs, past, D_KV),
                         cache_v_win[l].reshape(bs, past, D_KV), state_hgrn[l], state_conv[l], l)
        yp = _outproj(yp, mp[0].reshape(bp * seq, D_MODEL), modp[:, 1], g_post[1], w_mix_out,
                      PROMPT_TM, tiles_per_batch)
        ys = _outproj(ys, ms[0].reshape(bs * tq, D_MODEL), mods[1], g_post[1], w_mix_out,
                      SAMPLE_TM, tiles_per_batch)

        yp = ffn(yp, modp[:, 2], 2, 1, PROMPT_TM)
        ys = ffn(ys, mods[2], 2, 1, SAMPLE_TM)
        outs_p.append(mp[1:])
        outs_s.append(ms[1:])

    def stack(outs, i, shape):
        return jnp.stack([o[i] for o in outs]).reshape(shape)

    kv_p = (DEPTH, bp, WINDOW, N_KV_HEADS, HEAD_DIM)
    kv_s = (DEPTH, bs, past, N_KV_HEADS, HEAD_DIM)
    return (yp.reshape(bp, seq, D_MODEL), ys.reshape(bs, tq, D_MODEL),
            stack(outs_p, 0, kv_p), stack(outs_p, 1, kv_p),
            stack(outs_p, 2, (DEPTH, bp, HG_HEADS, HEAD_DIM, HEAD_DIM)),
            stack(outs_p, 3, (DEPTH, bp, CONV_WIDTH - 1, D_CONV)),
            stack(outs_s, 0, kv_s), stack(outs_s, 1, kv_s),
            stack(outs_s, 2, (DEPTH, bs, HG_HEADS, HEAD_DIM, HEAD_DIM)),
            stack(outs_s, 3, (DEPTH, bs, CONV_WIDTH - 1, D_CONV)))
```

```python
import functools

import jax
import jax.numpy as jnp
from jax import lax
from jax.experimental import pallas as pl
from jax.experimental.pallas import tpu as pltpu

F32 = jnp.float32
BF16 = jnp.bfloat16

D_MODEL = 1024
DEPTH = 2
N_SUB = 3
PAST_LEN = 16384
HEAD_DIM = 64
N_HEADS = 8
N_KV_HEADS = 2
GROUP = N_HEADS // N_KV_HEADS
WINDOW = 128
ROPE_THETA = 10000.0
D_ATTN = N_HEADS * HEAD_DIM
D_KV = N_KV_HEADS * HEAD_DIM
D_HG = 256
HG_HEADS = 4
HG_CHUNK = 64
HG_DIAG = 8
HG_LEVELS = (32, 16, 8)
D_CONV = 256
CONV_WIDTH = 3
D_IN = 2560
D_FF = 2816
FFN_RES_WEIGHT = 0.5
RMS_EPS = 1e-6
NEG_BIG = -1e30

OFF_ATT, OFF_HG, OFF_CV = 0, 768, 1792
W_ATT, W_HG, W_CV = 768, 1024, 768

FF_CHUNK = 256
VMEM_LIMIT = 56 * 1024 * 1024


def _sigmoid(x):
    return 1.0 / (1.0 + jnp.exp(-x))


def _rms(x, g):
    ms = jnp.mean(x * x, axis=-1, keepdims=True)
    return x * lax.rsqrt(ms + RMS_EPS) * g


def _dot(a, b):
    return jnp.dot(a, b, preferred_element_type=F32)


def _dot_nt(a, b):
    return lax.dot_general(a, b, (((1,), (1,)), ((), ())), preferred_element_type=F32)


def _dot_tn(a, b):
    return lax.dot_general(a, b, (((0,), (0,)), ((), ())), preferred_element_type=F32)


def _split3(x):
    hi = x.astype(BF16)
    r = x - hi.astype(F32)
    mid = r.astype(BF16)
    return hi, mid, (r - mid.astype(F32)).astype(BF16)


def _adaln_kernel(c_ref, w_ref, b_ref, o_ref):
    c = c_ref[...]
    sc = (c * _sigmoid(c)).astype(BF16)
    o_ref[...] = _dot(sc, w_ref[...].astype(BF16)) + b_ref[...]


def _adaln(c_all, ada_w, ada_b):
    nb = c_all.shape[0]
    nvec = ada_w.shape[-1] // D_MODEL
    return pl.pallas_call(
        _adaln_kernel,
        grid=(DEPTH, nvec),
        in_specs=[
            pl.BlockSpec((nb, D_MODEL), lambda l, n: (0, 0)),
            pl.BlockSpec((None, D_MODEL, D_MODEL), lambda l, n: (l, 0, n)),
            pl.BlockSpec((None, 1, D_MODEL), lambda l, n: (l, 0, n)),
        ],
        out_specs=pl.BlockSpec((None, None, nb, D_MODEL), lambda l, n: (l, n, 0, 0)),
        out_shape=jax.ShapeDtypeStruct((DEPTH, nvec, nb, D_MODEL), F32),
        compiler_params=pltpu.CompilerParams(
            dimension_semantics=("parallel", "parallel"), vmem_limit_bytes=VMEM_LIMIT),
        name="adaln",
    )(c_all, ada_w, ada_b.reshape(DEPTH, 1, nvec * D_MODEL))


def _mod_specs(layer, sub, bb, row0):
    def spec(k):
        return pl.BlockSpec((None, None, bb, 1, D_MODEL),
                            lambda b, t: (layer, N_SUB * sub + k, row0 // bb + b, 0, 0))
    return [spec(0), spec(1), spec(2)]


def _norm_spec(layer, sub):
    return pl.BlockSpec((None, None, 1, D_MODEL), lambda b, t: (layer, sub, 0, 0))


def _resident(block_shape, index):
    return pl.BlockSpec(block_shape, lambda b, t: index, pipeline_mode=pl.Buffered(1))


def _modulated(x_ref, sh_ref, sc_ref, gpre_ref):
    x3 = x_ref[...]
    bb, tt, d = x3.shape
    n3 = _rms(x3.reshape(bb * tt, d), gpre_ref[...]).reshape(bb, tt, d)
    return x3, (n3 * (1.0 + sc_ref[...]) + sh_ref[...]).reshape(bb * tt, d).astype(BF16)


def _residual(x3, out, gt_ref, gpost_ref, weight):
    return x3 + (weight * gt_ref[...]) * _rms(out, gpost_ref[...]).reshape(x3.shape)


def _token_call(kern, name, x, extra_in, extra_specs, layer, sub, bb, tt, row0, mod, out_width):
    nb, t_len, _ = x.shape
    return pl.pallas_call(
        kern,
        grid=(nb // bb, t_len // tt),
        in_specs=[pl.BlockSpec((bb, tt, D_MODEL), lambda b, t: (b, t, 0))]
        + _mod_specs(layer, sub, bb, row0) + extra_specs,
        out_specs=pl.BlockSpec((bb, tt, out_width), lambda b, t: (b, t, 0)),
        out_shape=jax.ShapeDtypeStruct((nb, t_len, out_width), F32),
        compiler_params=pltpu.CompilerParams(
            dimension_semantics=("parallel", "parallel"), vmem_limit_bytes=VMEM_LIMIT),
        name=name,
    )(x, mod, mod, mod, *extra_in)


def _ffn_kernel(x_ref, sh_ref, sc_ref, gt_ref, gpre_ref, gpost_ref, win_ref, wout_ref, o_ref):
    x3, hb = _modulated(x_ref, sh_ref, sc_ref, gpre_ref)
    acc = jnp.zeros((hb.shape[0], D_MODEL), F32)
    for c in range(D_FF // FF_CHUNK):
        lo = c * FF_CHUNK
        g = _dot(hb, win_ref[:, lo:lo + FF_CHUNK])
        u = _dot(hb, win_ref[:, D_FF + lo:D_FF + lo + FF_CHUNK])
        a = (g * _sigmoid(g) * u).astype(BF16)
        acc = acc + _dot(a, wout_ref[lo:lo + FF_CHUNK, :])
    o_ref[...] = _residual(x3, acc, gt_ref, gpost_ref, FFN_RES_WEIGHT)


def _ffn(x, mod, g_pre, g_post, w_in, w_out, layer, sub, which, bb, tt, row0):
    specs = [_norm_spec(layer, sub), _norm_spec(layer, sub),
             _resident((None, None, D_MODEL, 2 * D_FF), (layer, which, 0, 0)),
             _resident((None, None, D_FF, D_MODEL), (layer, which, 0, 0))]
    return _token_call(_ffn_kernel, "ffn", x, (g_pre, g_post, w_in, w_out), specs,
                       layer, sub, bb, tt, row0, mod, D_MODEL)


def _inproj_kernel(x_ref, sh_ref, sc_ref, gt_ref, gpre_ref, w_ref, z_ref):
    _, hb = _modulated(x_ref, sh_ref, sc_ref, gpre_ref)
    z_ref[...] = _dot(hb, w_ref[...]).reshape(z_ref.shape)


def _inproj(x, mod, g_pre, w_in, layer, bb, tt, row0):
    specs = [_norm_spec(layer, 1), _resident((None, D_MODEL, D_IN), (layer, 0, 0))]
    return _token_call(_inproj_kernel, "mix_in", x, (g_pre, w_in), specs,
                       layer, 1, bb, tt, row0, mod, D_IN)


def _outproj_kernel(x_ref, sh_ref, sc_ref, gt_ref, m_ref, gpost_ref, w_ref, o_ref):
    x3 = x_ref[...]
    bb, tt, d = x3.shape
    out = _dot(m_ref[...].reshape(bb * tt, d).astype(BF16), w_ref[...])
    o_ref[...] = _residual(x3, out, gt_ref, gpost_ref, 1.0)


def _outproj(x, mixed, mod, g_post, w_out, layer, bb, tt, row0):
    specs = [pl.BlockSpec((bb, tt, D_MODEL), lambda b, t: (b, t, 0)), _norm_spec(layer, 1),
             _resident((None, D_MODEL, D_MODEL), (layer, 0, 0))]
    return _token_call(_outproj_kernel, "mix_out", x, (mixed, g_post, w_out), specs,
                       layer, 1, bb, tt, row0, mod, D_MODEL)


def _rope_tables(pos):
    half = HEAD_DIM // 2
    inv = jnp.power(ROPE_THETA, -jnp.arange(half, dtype=F32) / half)
    ang = pos.astype(F32)[:, None] * inv[None, :]
    cos, sin = jnp.cos(ang), jnp.sin(ang)
    cs = jnp.tile(jnp.concatenate([cos, cos], axis=-1), (1, N_HEADS))
    sn = jnp.tile(jnp.concatenate([-sin, sin], axis=-1), (1, N_HEADS))
    return cs, sn


def _rope(x, cs, sn):
    w = x.shape[-1]
    lane = lax.broadcasted_iota(jnp.int32, x.shape, 1)
    swapped = jnp.where((lane & (HEAD_DIM - 1)) < HEAD_DIM // 2,
                        pltpu.roll(x, w - HEAD_DIM // 2, 1), pltpu.roll(x, HEAD_DIM // 2, 1))
    return x * cs + swapped * sn


def _qkv(za, cs, sn):
    q = (_rope(za[:, :D_ATTN], cs, sn) * (HEAD_DIM ** -0.5)).astype(BF16)
    k = _rope(za[:, D_ATTN:D_ATTN + D_KV], cs[:, :D_KV], sn[:, :D_KV])
    return q, k, za[:, D_ATTN + D_KV:]


def _lower_bound(lbraw, layer):
    rows = [lbraw[i:i + 1, :] for i in range(DEPTH)]
    mx = functools.reduce(jnp.maximum, rows)
    es = [jnp.exp(r - mx) for r in rows]
    tot = functools.reduce(lambda a, b: a + b, es)
    sm = [e / tot for e in es]
    acc = sm[0]
    for i in range(1, layer + 1):
        acc = acc + sm[i]
    return acc - sm[0]


def _div_pow2(x, n):
    assert n & (n - 1) == 0
    return x >> (n.bit_length() - 1)


def _same_head(nrows, rows_per_head):
    r = _div_pow2(lax.broadcasted_iota(jnp.int32, (nrows, D_HG), 0), rows_per_head)
    c = _div_pow2(lax.broadcasted_iota(jnp.int32, (nrows, D_HG), 1), HEAD_DIM)
    return r == c


def _head_sum(w, head_ones):
    return _dot(w.astype(BF16), head_ones)


def _roll8(x, d):
    m, w = x.shape
    return pltpu.roll(x.reshape(m // HG_DIAG, HG_DIAG, w), d, 1).reshape(m, w)


def _cumsum_chunks(x, period):
    m = x.shape[0]
    r = lax.broadcasted_iota(jnp.int32, (m, m), 0)
    c = lax.broadcasted_iota(jnp.int32, (m, m), 1)
    tri = jnp.where((c <= r) & (_div_pow2(r, period) == _div_pow2(c, period)), 1.0, 0.0).astype(BF16)
    hi, mid, lo = _split3(x)
    return _dot(tri, hi) + _dot(tri, mid) + _dot(tri, lo)


def _cumsum_scan(x):
    row = lax.broadcasted_iota(jnp.int32, x.shape, 0)
    s = 1
    while s < x.shape[0]:
        x = x + jnp.where(row >= s, pltpu.roll(x, s, 0), 0.0)
        s *= 2
    return x


def _hgrn_diag_weights(qh, kk, f, head_ones):
    rmod = lax.broadcasted_iota(jnp.int32, qh.shape, 0) & (HG_DIAG - 1)
    out = [_head_sum(qh * kk, head_ones)]
    prod = None
    for d in range(1, HG_DIAG):
        fr = f if d == 1 else _roll8(f, d - 1)
        prod = fr if prod is None else prod * fr
        w = jnp.where(rmod >= d, qh * _roll8(kk, d) * prod, 0.0)
        out.append(_head_sum(w, head_ones))
    return out


def _hgrn_chunk(qh, kk, b, vv, state, a_diag):
    rows = qh.shape[0]
    o = _dot((qh * jnp.exp(b)).astype(BF16), state.astype(BF16))
    if a_diag is not None:
        row = lax.broadcasted_iota(jnp.int32, (rows, D_HG), 0)
        col_s = lax.broadcasted_iota(jnp.int32, (rows, 4 * rows), 1) & (rows - 1)
        row_a = lax.broadcasted_iota(jnp.int32, (rows, 4 * rows), 0)
        bd = _same_head(4 * rows, rows)
        a = a_diag
        for m in HG_LEVELS:
            ref = jnp.concatenate(
                [jnp.broadcast_to(b[i * 2 * m + m - 1:i * 2 * m + m, :], (2 * m, D_HG))
                 for i in range(rows // (2 * m))], axis=0)
            upper = (row & (2 * m - 1)) >= m
            qt = jnp.where(upper, qh * jnp.exp(jnp.minimum(b - ref, 0.0)), 0.0)
            kt = jnp.where(upper, 0.0, kk * jnp.exp(jnp.minimum(ref - b, 0.0)))
            kbd = jnp.where(bd, jnp.concatenate([kt] * 4, axis=0), 0.0).astype(BF16)
            am = _dot_nt(qt.astype(BF16), kbd)
            a = a + jnp.where(_div_pow2(row_a, 2 * m) == _div_pow2(col_s, 2 * m), am, 0.0)
        vbd = jnp.where(bd, jnp.concatenate([vv] * 4, axis=0), 0.0).astype(BF16)
        o = o + _dot(a.astype(BF16), vbd)
    b_last = b[rows - 1:rows, :]
    khat = kk * jnp.exp(b_last - b)
    upd = _dot_tn(khat.astype(BF16), vv.astype(BF16))
    sq = (D_HG, D_HG)
    eye = lax.broadcasted_iota(jnp.int32, sq, 0) == lax.broadcasted_iota(jnp.int32, sq, 1)
    decay_col = jnp.sum(jnp.where(eye, jnp.broadcast_to(jnp.exp(b_last), sq), 0.0),
                        axis=1, keepdims=True)
    new_state = decay_col * state + jnp.where(_same_head(D_HG, HEAD_DIM), upd, 0.0)
    return o, new_state


def _hgrn_gates(zh, lb):
    f = lb + (1.0 - lb) * _sigmoid(zh[:, D_HG:2 * D_HG])
    return zh[:, :D_HG], 1.0 - f, f, jnp.log(f), zh[:, 2 * D_HG:3 * D_HG]


def _hgrn_finish(o, gh, gn, head_ones):
    sq = o * o
    hi = sq.astype(BF16)
    lo = (sq - hi.astype(F32)).astype(BF16)
    ms = (_dot(hi, head_ones) + _dot(lo, head_ones)) * (1.0 / HEAD_DIM)
    return o * lax.rsqrt(ms + RMS_EPS) * gn * (gh * _sigmoid(gh))


def _softmax_pv(parts, sink):
    masked = [jnp.where(msk, s, NEG_BIG) for s, msk, _ in parts]
    m = sink
    for s in masked:
        m = jnp.maximum(m, jnp.max(s, axis=-1, keepdims=True))
    den = jnp.exp(sink - m)
    acc = None
    for s, (_, _, v) in zip(masked, parts):
        p = jnp.exp(s - m)
        den = den + jnp.sum(p, axis=-1, keepdims=True)
        pv = _dot(p.astype(BF16), v)
        acc = pv if acc is None else acc + pv
    return acc * (1.0 / den)


def _mixp_kernel(sink_ref, x_ref, sh_ref, sc_ref, gt_ref, gpre_ref, gpost_ref, win_ref, wout_ref,
                 cs_ref, sn_ref, lb_ref, gn_ref, cw_ref,
                 o_ref, nk_ref, nv_ref, ns_ref, nc_ref,
                 kbuf, vbuf, sbuf, ubuf, abuf, *, layer, tt, nt):
    t = pl.program_id(1)

    @pl.when(t == 0)
    def _():
        kbuf[0:WINDOW, :] = jnp.zeros((WINDOW, D_KV), BF16)
        vbuf[0:WINDOW, :] = jnp.zeros((WINDOW, D_KV), BF16)
        sbuf[...] = jnp.zeros(sbuf.shape, F32)
        ubuf[0:8, :] = jnp.zeros((8, D_CONV), F32)

    x3, hb = _modulated(x_ref, sh_ref, sc_ref, gpre_ref)
    za = _dot(hb, win_ref[:, OFF_ATT:OFF_ATT + W_ATT])
    zh = _dot(hb, win_ref[:, OFF_HG:OFF_HG + W_HG])
    zc = _dot(hb, win_ref[:, OFF_CV:OFF_CV + W_CV])

    q, k, v = _qkv(za, cs_ref[...], sn_ref[...])
    kb, vb = k.astype(BF16), v.astype(BF16)
    kbuf[WINDOW:WINDOW + tt, :] = kb
    vbuf[WINDOW:WINDOW + tt, :] = vb
    qi = lax.broadcasted_iota(jnp.int32, (WINDOW, 2 * WINDOW), 0)
    kj = lax.broadcasted_iota(jnp.int32, (WINDOW, 2 * WINDOW), 1)
    band = (kj > qi) & (kj <= qi + WINDOW)
    first_kmin = jnp.where(t == 0, WINDOW, 0)
    for j in range(tt // WINDOW):
        mask = (band & (kj >= first_kmin)) if j == 0 else band
        r0 = j * WINDOW
        for hk in range(N_KV_HEADS):
            kk_ = kbuf[r0:r0 + 2 * WINDOW, hk * HEAD_DIM:(hk + 1) * HEAD_DIM]
            vv_ = vbuf[r0:r0 + 2 * WINDOW, hk * HEAD_DIM:(hk + 1) * HEAD_DIM]
            for g in range(GROUP):
                h = hk * GROUP + g
                s = _dot_nt(q[r0:r0 + WINDOW, h * HEAD_DIM:(h + 1) * HEAD_DIM], kk_)
                abuf[r0:r0 + WINDOW, h * HEAD_DIM:(h + 1) * HEAD_DIM] = _softmax_pv(
                    [(s, mask, vv_)], sink_ref[layer, h])
    out = _dot(abuf[...].astype(BF16), wout_ref[0:D_ATTN, :])

    head_ones = _same_head(D_HG, HEAD_DIM).astype(BF16)
    qh, kk, f, lf, vv = _hgrn_gates(zh, _lower_bound(lb_ref[...], layer))
    b = _cumsum_chunks(lf, HG_CHUNK)
    shift = ((lax.broadcasted_iota(jnp.int32, (tt, D_HG), 0) & (HG_CHUNK - 1))
             - (lax.broadcasted_iota(jnp.int32, (tt, D_HG), 1) & (HG_CHUNK - 1)))
    a_diag = jnp.zeros((tt, D_HG), F32)
    for d, wd in enumerate(_hgrn_diag_weights(qh, kk, f, head_ones)):
        a_diag = jnp.where(shift == d, wd, a_diag)
    state = sbuf[...]
    outs = []
    for c in range(tt // HG_CHUNK):
        sl = slice(c * HG_CHUNK, (c + 1) * HG_CHUNK)
        o_c, state = _hgrn_chunk(qh[sl], kk[sl], b[sl], vv[sl], state, a_diag[sl])
        outs.append(o_c)
    sbuf[...] = state
    o_hg = _hgrn_finish(jnp.concatenate(outs, axis=0), zh[:, 3 * D_HG:], gn_ref[...], head_ones)
    out = out + _dot(o_hg.astype(BF16), wout_ref[D_ATTN:D_ATTN + D_HG, :])

    u = zc[:, D_CONV:2 * D_CONV] * zc[:, 2 * D_CONV:]
    ubuf[8:8 + tt, :] = u
    cw = cw_ref[...]
    y = ubuf[6:6 + tt, :] * cw[0:1, :] + ubuf[7:7 + tt, :] * cw[1:2, :] + u * cw[2:3, :]
    out = out + _dot((zc[:, :D_CONV] * y).astype(BF16), wout_ref[D_ATTN + D_HG:, :])

    o_ref[...] = _residual(x3, out, gt_ref, gpost_ref, 1.0)

    kbuf[0:WINDOW, :] = kb[tt - WINDOW:, :]
    vbuf[0:WINDOW, :] = vb[tt - WINDOW:, :]
    ubuf[0:8, :] = u[tt - 8:, :]

    @pl.when(t == nt - 1)
    def _():
        nk_ref[...] = k[tt - WINDOW:, :]
        nv_ref[...] = v[tt - WINDOW:, :]
        nc_ref[...] = u[tt - (CONV_WIDTH - 1):, :]
        for h in range(HG_HEADS):
            ns_ref[h] = state[h * HEAD_DIM:(h + 1) * HEAD_DIM, h * HEAD_DIM:(h + 1) * HEAD_DIM]


def _mix_prompt(x, mod, g_pre, g_post, w_in, w_out, cs, sn, sinks, lbraw, gn, cw, layer, tt, row0):
    bsz, seq, _ = x.shape
    nt = seq // tt
    kern = functools.partial(_mixp_kernel, layer=layer, tt=tt, nt=nt)
    per_b = lambda b, t: (b, 0, 0)
    return pl.pallas_call(
        kern,
        grid=(bsz, nt),
        in_specs=[pl.BlockSpec(memory_space=pltpu.SMEM),
                  pl.BlockSpec((1, tt, D_MODEL), lambda b, t: (b, t, 0))]
        + _mod_specs(layer, 1, 1, row0)
        + [_norm_spec(layer, 1), _norm_spec(layer, 1),
           _resident((None, D_MODEL, D_IN), (layer, 0, 0)),
           _resident((None, D_MODEL, D_MODEL), (layer, 0, 0)),
           pl.BlockSpec((tt, D_ATTN), lambda b, t: (t, 0)),
           pl.BlockSpec((tt, D_ATTN), lambda b, t: (t, 0)),
           pl.BlockSpec((DEPTH, D_HG), lambda b, t: (0, 0)),
           pl.BlockSpec((None, 1, D_HG), lambda b, t: (layer, 0, 0)),
           pl.BlockSpec((None, CONV_WIDTH, D_CONV), lambda b, t: (layer, 0, 0))],
        out_specs=[
            pl.BlockSpec((1, tt, D_MODEL), lambda b, t: (b, t, 0)),
            pl.BlockSpec((None, WINDOW, D_KV), per_b),
            pl.BlockSpec((None, WINDOW, D_KV), per_b),
            pl.BlockSpec((None, HG_HEADS, HEAD_DIM, HEAD_DIM), lambda b, t: (b, 0, 0, 0)),
            pl.BlockSpec((None, CONV_WIDTH - 1, D_CONV), per_b),
        ],
        out_shape=[
            jax.ShapeDtypeStruct((bsz, seq, D_MODEL), F32),
            jax.ShapeDtypeStruct((bsz, WINDOW, D_KV), F32),
            jax.ShapeDtypeStruct((bsz, WINDOW, D_KV), F32),
            jax.ShapeDtypeStruct((bsz, HG_HEADS, HEAD_DIM, HEAD_DIM), F32),
            jax.ShapeDtypeStruct((bsz, CONV_WIDTH - 1, D_CONV), F32),
        ],
        scratch_shapes=[
            pltpu.VMEM((WINDOW + tt, D_KV), BF16),
            pltpu.VMEM((WINDOW + tt, D_KV), BF16),
            pltpu.VMEM((D_HG, D_HG), F32),
            pltpu.VMEM((8 + tt, D_CONV), F32),
            pltpu.VMEM((tt, D_ATTN), F32),
        ],
        compiler_params=pltpu.CompilerParams(
            dimension_semantics=("parallel", "arbitrary"), vmem_limit_bytes=VMEM_LIMIT),
        name="mix_prompt",
    )(sinks, x, mod, mod, mod, g_pre, g_post, w_in, w_out, cs, sn, lbraw, gn, cw)


SAMPLE_ROWS = 16


def _pad_rows(x, fill):
    return jnp.concatenate([x, jnp.full((SAMPLE_ROWS - x.shape[0], x.shape[1]), fill, x.dtype)],
                           axis=0)


def _mixs_one(bi, sink_ref, z_ref, cs, sn, lb, gn, cw, ck_ref, cv_ref, s0_ref, c0_ref,
              mix_ref, nk_ref, nv_ref, ns_ref, nc_ref, ubuf, *, layer, tq, past):
    z = z_ref[bi]

    q, k, v = _qkv(z[:, OFF_ATT:OFF_ATT + W_ATT], cs, sn)
    ck, cv = ck_ref[bi], cv_ref[bi]
    nk_ref[bi, 0:past - tq, :] = ck[tq:, :]
    nk_ref[bi, past - tq:, :] = k
    nv_ref[bi, 0:past - tq, :] = cv[tq:, :]
    nv_ref[bi, past - tq:, :] = v
    ckb, cvb = ck.astype(BF16), cv.astype(BF16)
    kpb, vpb = _pad_rows(k, 0.0).astype(BF16), _pad_rows(v, 0.0).astype(BF16)

    mrows = GROUP * tq
    qi_c = lax.broadcasted_iota(jnp.int32, (mrows, past), 0) & (tq - 1)
    kj_c = lax.broadcasted_iota(jnp.int32, (mrows, past), 1)
    mask_c = kj_c > qi_c + (past - WINDOW)
    qi_n = lax.broadcasted_iota(jnp.int32, (mrows, SAMPLE_ROWS), 0) & (tq - 1)
    kj_n = lax.broadcasted_iota(jnp.int32, (mrows, SAMPLE_ROWS), 1)
    mask_n = kj_n <= qi_n
    grp = _div_pow2(lax.broadcasted_iota(jnp.int32, (mrows, 1), 0), tq)
    for hk in range(N_KV_HEADS):
        hs = slice(hk * HEAD_DIM, (hk + 1) * HEAD_DIM)
        qs = jnp.concatenate(
            [q[:, (hk * GROUP + g) * HEAD_DIM:(hk * GROUP + g + 1) * HEAD_DIM]
             for g in range(GROUP)], axis=0)
        sink = jnp.zeros((mrows, 1), F32)
        for g in range(GROUP):
            sink = jnp.where(grp == g, sink_ref[layer, hk * GROUP + g], sink)
        o = _softmax_pv([(_dot_nt(qs, ckb[:, hs]), mask_c, cvb[:, hs]),
                         (_dot_nt(qs, kpb[:, hs]), mask_n, vpb[:, hs])], sink)
        for g in range(GROUP):
            h = hk * GROUP + g
            mix_ref[bi, :, h * HEAD_DIM:(h + 1) * HEAD_DIM] = o[g * tq:(g + 1) * tq, :]

    head_ones = _same_head(D_HG, HEAD_DIM).astype(BF16)
    zh = z[:, OFF_HG:OFF_HG + W_HG]
    qh, kk, f, lf, vv = _hgrn_gates(zh, lb)
    qh, kk, lf, vv = (_pad_rows(a, 0.0) for a in (qh, kk, lf, vv))
    f = _pad_rows(f, 1.0)
    s0 = s0_ref[bi].reshape(D_HG, HEAD_DIM)
    state = jnp.where(_same_head(D_HG, HEAD_DIM), jnp.concatenate([s0] * HG_HEADS, axis=1), 0.0)
    o_hg, state = _hgrn_chunk(qh, kk, _cumsum_scan(lf), vv, state, None)
    for d, wd in enumerate(_hgrn_diag_weights(qh, kk, f, head_ones)):
        o_hg = o_hg + wd * (vv if d == 0 else _roll8(vv, d))
    mix_ref[bi, :, D_ATTN:D_ATTN + D_HG] = _hgrn_finish(o_hg[0:tq, :], zh[:, 3 * D_HG:], gn, head_ones)
    for h in range(HG_HEADS):
        ns_ref[bi, h] = state[h * HEAD_DIM:(h + 1) * HEAD_DIM, h * HEAD_DIM:(h + 1) * HEAD_DIM]

    zc = z[:, OFF_CV:OFF_CV + W_CV]
    u = zc[:, D_CONV:2 * D_CONV] * zc[:, 2 * D_CONV:]
    ubuf[bi, 0:6, :] = jnp.zeros((6, D_CONV), F32)
    ubuf[bi, 6:8, :] = c0_ref[bi]
    ubuf[bi, 8:8 + tq, :] = u
    y = (ubuf[bi, 6:6 + tq, :] * cw[0:1, :] + ubuf[bi, 7:7 + tq, :] * cw[1:2, :] + u * cw[2:3, :])
    mix_ref[bi, :, D_ATTN + D_HG:] = zc[:, :D_CONV] * y
    nc_ref[bi] = u[tq - (CONV_WIDTH - 1):, :]


def _mixs_kernel(sink_ref, z_ref, cs_ref, sn_ref, lb_ref, gn_ref, cw_ref, *refs, layer, tq, past, bb):
    cs, sn, gn, cw = cs_ref[...], sn_ref[...], gn_ref[...], cw_ref[...]
    lb = _lower_bound(lb_ref[...], layer)
    for bi in range(bb):
        _mixs_one(bi, sink_ref, z_ref, cs, sn, lb, gn, cw, *refs, layer=layer, tq=tq, past=past)


def _mix_sample(z, cs, sn, sinks, lbraw, gn, cw, cache_k, cache_v, s0, c0, layer, bb):
    bsz, tq, _ = z.shape
    past = cache_k.shape[2]
    kern = functools.partial(_mixs_kernel, layer=layer, tq=tq, past=past, bb=bb)
    per_b = lambda b: (b, 0, 0)
    lay_b = lambda b: (layer, b, 0, 0)
    shared = lambda b: (0, 0)
    return pl.pallas_call(
        kern,
        grid=(bsz // bb,),
        in_specs=[
            pl.BlockSpec(memory_space=pltpu.SMEM),
            pl.BlockSpec((bb, tq, D_IN), per_b),
            pl.BlockSpec((tq, D_ATTN), shared),
            pl.BlockSpec((tq, D_ATTN), shared),
            pl.BlockSpec((DEPTH, D_HG), shared),
            pl.BlockSpec((None, 1, D_HG), lambda b: (layer, 0, 0)),
            pl.BlockSpec((None, CONV_WIDTH, D_CONV), lambda b: (layer, 0, 0)),
            pl.BlockSpec((None, bb, past, D_KV), lay_b),
            pl.BlockSpec((None, bb, past, D_KV), lay_b),
            pl.BlockSpec((None, bb, HG_HEADS, HEAD_DIM, HEAD_DIM), lambda b: (layer, b, 0, 0, 0)),
            pl.BlockSpec((None, bb, CONV_WIDTH - 1, D_CONV), lay_b),
        ],
        out_specs=[
            pl.BlockSpec((bb, tq, D_MODEL), per_b),
            pl.BlockSpec((bb, past, D_KV), per_b),
            pl.BlockSpec((bb, past, D_KV), per_b),
            pl.BlockSpec((bb, HG_HEADS, HEAD_DIM, HEAD_DIM), lambda b: (b, 0, 0, 0)),
            pl.BlockSpec((bb, CONV_WIDTH - 1, D_CONV), per_b),
        ],
        out_shape=[
            jax.ShapeDtypeStruct((bsz, tq, D_MODEL), F32),
            jax.ShapeDtypeStruct((bsz, past, D_KV), F32),
            jax.ShapeDtypeStruct((bsz, past, D_KV), F32),
            jax.ShapeDtypeStruct((bsz, HG_HEADS, HEAD_DIM, HEAD_DIM), F32),
            jax.ShapeDtypeStruct((bsz, CONV_WIDTH - 1, D_CONV), F32),
        ],
        scratch_shapes=[pltpu.VMEM((bb, 8 + tq, D_CONV), F32)],
        compiler_params=pltpu.CompilerParams(
            dimension_semantics=("parallel",), vmem_limit_bytes=VMEM_LIMIT),
        name="mix_sample",
    )(sinks, z, cs, sn, lbraw, gn, cw, cache_k, cache_v, s0, c0)


PROMPT_TM = 512
PROMPT_TT = 256
SAMPLE_BB = 32
SAMPLE_MIX_BB = 8


def kernel(x_prompt, x_sample, cache_k_win, cache_v_win, state_hgrn, state_conv, c_prompt, c_sample,
           ada_w, ada_b, norm_pre, norm_post, ffn_w_in, ffn_w_out, mix_w_in, mix_w_out,
           attn_sinks, hgrn_lower_bounds, hgrn_gnorm, conv_w):
    bp, seq, _ = x_prompt.shape
    bs, tq, _ = x_sample.shape
    past = cache_k_win.shape[2]

    mod = _adaln(jnp.concatenate([c_sample, c_prompt], axis=0), ada_w, ada_b)
    mod = mod.reshape(DEPTH, N_SUB * 3, bs + bp, 1, D_MODEL)
    row_s, row_p = 0, bs
    cs_p, sn_p = _rope_tables(jnp.arange(seq))
    cs_s, sn_s = _rope_tables(PAST_LEN + jnp.arange(tq))
    gn = jnp.tile(hgrn_gnorm, (1, HG_HEADS)).reshape(DEPTH, 1, D_HG)
    g_pre = norm_pre.reshape(DEPTH, N_SUB, 1, D_MODEL)
    g_post = norm_post.reshape(DEPTH, N_SUB, 1, D_MODEL)
    w_fi, w_fo = ffn_w_in.astype(BF16), ffn_w_out.astype(BF16)
    w_mi, w_mo = mix_w_in.astype(BF16), mix_w_out.astype(BF16)
    ck = cache_k_win.reshape(DEPTH, bs, past, D_KV)
    cv = cache_v_win.reshape(DEPTH, bs, past, D_KV)

    yp, ys = x_prompt, x_sample
    outs_p, outs_s = [], []
    for l in range(DEPTH):
        yp = _ffn(yp, mod, g_pre, g_post, w_fi, w_fo, l, 0, 0, 1, PROMPT_TM, row_p)
        ys = _ffn(ys, mod, g_pre, g_post, w_fi, w_fo, l, 0, 0, SAMPLE_BB, tq, row_s)

        yp, *op = _mix_prompt(yp, mod, g_pre, g_post, w_mi, w_mo, cs_p, sn_p, attn_sinks,
                              hgrn_lower_bounds, gn, conv_w, l, PROMPT_TT, row_p)
        zs = _inproj(ys, mod, g_pre, w_mi, l, SAMPLE_BB, tq, row_s)
        ms, *os_ = _mix_sample(zs, cs_s, sn_s, attn_sinks, hgrn_lower_bounds, gn, conv_w,
                               ck, cv, state_hgrn, state_conv, l, SAMPLE_MIX_BB)
        ys = _outproj(ys, ms, mod, g_post, w_mo, l, SAMPLE_BB, tq, row_s)

        yp = _ffn(yp, mod, g_pre, g_post, w_fi, w_fo, l, 2, 1, 1, PROMPT_TM, row_p)
        ys = _ffn(ys, mod, g_pre, g_post, w_fi, w_fo, l, 2, 1, SAMPLE_BB, tq, row_s)
        outs_p.append(op)
        outs_s.append(os_)

    def stack(outs, i, shape):
        return jnp.stack([o[i] for o in outs]).reshape(shape)

    kv_p = (DEPTH, bp, WINDOW, N_KV_HEADS, HEAD_DIM)
    kv_s = (DEPTH, bs, past, N_KV_HEADS, HEAD_DIM)
    return (yp, ys,
            stack(outs_p, 0, kv_p), stack(outs_p, 1, kv_p),
            stack(outs_p, 2, (DEPTH, bp, HG_HEADS, HEAD_DIM, HEAD_DIM)),
            stack(outs_p, 3, (DEPTH, bp, CONV_WIDTH - 1, D_CONV)),
            stack(outs_s, 0, kv_s), stack(outs_s, 1, kv_s),
            stack(outs_s, 2, (DEPTH, bs, HG_HEADS, HEAD_DIM, HEAD_DIM)),
            stack(outs_s, 3, (DEPTH, bs, CONV_WIDTH - 1, D_CONV)))
```
